```python
import math
import jax, jax.numpy as jnp
from jax import lax
import numpy as np

D_MODEL = 1024
BATCH = 4
SEQ = 4096
DEPTH = 1

D_MIX = D_MODEL
W_A = D_MIX // 2
W_B = D_MIX - W_A
CHUNK = 128
HEADS_A = 8
DH_A = W_A // HEADS_A
SPATIAL_INIT_STD = 0.02
S5_GROUP = 16
S5_GROUPS = W_B // S5_GROUP
S5_STATE = 64
DT_MIN = 1e-3
DT_MAX = 1e-1
N_EXPERT_GROUPS = 4
EXPERTS_PER_GROUP = 8
TOP_K = 2
D_EXPERT = 256
PLE_DIM = 256
EPS = 1e-6

kernel_name = "hymba_gmlp_s5_hmoe_block"


def rmsnorm(x, g):
    xf = x.astype(jnp.float32)
    y = xf * lax.rsqrt(jnp.mean(xf * xf, axis=-1, keepdims=True) + EPS)
    return (y * g.astype(jnp.float32)).astype(x.dtype)


def gmlp_mixer(u, v, sgu_gain, w_s, b_s):
    bsz, seq, _ = u.shape
    u = jax.nn.gelu(u)
    v = rmsnorm(jax.nn.gelu(v), sgu_gain)
    v = v.reshape(bsz, seq // CHUNK, CHUNK, HEADS_A, DH_A)
    mask = jnp.tril(jnp.ones((CHUNK, CHUNK), dtype=v.dtype))
    z = jnp.einsum('hts,bcshd->bcthd', w_s * mask, v) + b_s.T[None, None, :, :, None]
    return u * z.reshape(bsz, seq, W_A)


def s5_mixer(s, a_re, a_im, log_dt, b_re, b_im, c_re, c_im, d_skip, w_glu, b_glu):
    dtype = s.dtype
    f32 = jnp.float32
    bsz, seq, _ = s.shape
    sf = s.astype(f32)
    lam = lax.complex(a_re.astype(f32), a_im.astype(f32))
    dt = jnp.exp(log_dt.astype(f32))[:, None]
    a_bar = jnp.exp(lam * dt)
    b_mat = lax.complex(b_re.astype(f32), b_im.astype(f32))
    b_bar = ((a_bar - 1.0) / lam)[..., None] * b_mat
    ug = sf.reshape(bsz, seq, S5_GROUPS, S5_GROUP).astype(jnp.complex64)
    bu = jnp.einsum('gpc,bsgc->bsgp', b_bar, ug)
    a_seq = jnp.broadcast_to(a_bar, bu.shape)

    def combine(left, right):
        a_l, b_l = left
        a_r, b_r = right
        return a_r * a_l, a_r * b_l + b_r

    _, states = lax.associative_scan(combine, (a_seq, bu), axis=1)
    c_mat = lax.complex(c_re.astype(f32), c_im.astype(f32))
    y = jnp.einsum('gcp,bsgp->bsgc', c_mat, states).real.reshape(bsz, seq, W_B)
    y = jax.nn.gelu(y + d_skip.astype(f32) * sf)
    gl = y @ w_glu.astype(f32) + b_glu.astype(f32)
    out = gl[..., :W_B] * jax.nn.sigmoid(gl[..., W_B:])
    return out.astype(dtype)


def hier_moe(xn, w_coarse, b_coarse, w_fine, b_fine, w_gate_e, w_up_e, w_down_e):
    bsz, seq, d = xn.shape
    n = bsz * seq
    xt = xn.reshape(n, d)
    xf = xt.astype(jnp.float32)
    pc = jax.nn.softmax(xf @ w_coarse.astype(jnp.float32) + b_coarse.astype(jnp.float32), axis=-1)
    g_sel = jnp.argmax(pc, axis=-1)
    p_g = jnp.max(pc, axis=-1)
    lf_all = jnp.einsum('nd,gde->nge', xf, w_fine.astype(jnp.float32)) + b_fine.astype(jnp.float32)
    lf = jnp.take_along_axis(lf_all, g_sel[:, None, None], axis=1)[:, 0]
    pf = jax.nn.softmax(lf, axis=-1)
    top_v, top_i = lax.top_k(pf, TOP_K)
    top_w = top_v / jnp.sum(top_v, axis=-1, keepdims=True) * p_g[:, None]
    comb_e = jnp.sum(jax.nn.one_hot(top_i, EXPERTS_PER_GROUP, dtype=jnp.float32) * top_w[..., None], axis=1)
    comb = (jax.nn.one_hot(g_sel, N_EXPERT_GROUPS, dtype=jnp.float32)[:, :, None] * comb_e[:, None, :]).astype(xt.dtype)
    y = jnp.zeros((n, d), dtype=xt.dtype)
    for g in range(N_EXPERT_GROUPS):
        h = jax.nn.silu(jnp.einsum('nd,edf->nef', xt, w_gate_e[g])) * jnp.einsum('nd,edf->nef', xt, w_up_e[g])
        y = y + jnp.einsum('nef,efd->nd', h * comb[:, g, :, None], w_down_e[g])
    return y.reshape(bsz, seq, d)


def setup_inputs(seed: int = 0) -> dict:
    key = jax.random.key(seed)
    ks = iter(jax.random.split(key, 40))
    L, D = DEPTH, D_MODEL
    G, P, C = S5_GROUPS, S5_STATE, S5_GROUP
    NG, EPG, F = N_EXPERT_GROUPS, EXPERTS_PER_GROUP, D_EXPERT
    nrm = lambda shape, std: jax.random.normal(next(ks), shape, jnp.float32) * std
    gain = lambda shape: 1.0 + nrm(shape, 0.02)
    n_idx = jnp.arange(P, dtype=jnp.float32)
    return {
        "x": nrm((BATCH, SEQ, D), 1.0),
        "p": nrm((L, BATCH, SEQ, PLE_DIM), 1.0),
        "norm1": gain((L, D)),
        "w_in": nrm((L, D, 2 * W_A + W_B), D ** -0.5),
        "sgu_norm": gain((L, W_A)),
        "w_spatial": nrm((L, HEADS_A, CHUNK, CHUNK), SPATIAL_INIT_STD),
        "b_spatial": gain((L, HEADS_A, CHUNK)),
        "a_re": -0.5 + nrm((L, G, P), 0.01),
        "a_im": math.pi * n_idx[None, None, :] + nrm((L, G, P), 0.01),
        "log_dt": jax.random.uniform(next(ks), (L, G), jnp.float32, math.log(DT_MIN), math.log(DT_MAX)),
        "b_re": nrm((L, G, P, C), (2 * C) ** -0.5),
        "b_im": nrm((L, G, P, C), (2 * C) ** -0.5),
        "c_re": nrm((L, G, C, P), (2 * P) ** -0.5),
        "c_im": nrm((L, G, C, P), (2 * P) ** -0.5),
        "d_skip": nrm((L, W_B), 1.0),
        "w_glu": nrm((L, W_B, 2 * W_B), W_B ** -0.5),
        "b_glu": nrm((L, 2 * W_B), 0.02),
        "out_norm_a": gain((L, W_A)),
        "out_norm_b": gain((L, W_B)),
        "w_out": nrm((L, D_MIX, D), D_MIX ** -0.5),
        "norm2": gain((L, D)),
        "w_coarse": nrm((L, D, NG), D ** -0.5),
        "b_coarse": nrm((L, NG), 0.01),
        "w_fine": nrm((L, NG, D, EPG), D ** -0.5),
        "b_fine": nrm((L, NG, EPG), 0.01),
        "w_gate_e": nrm((L, NG, EPG, D, F), D ** -0.5),
        "w_up_e": nrm((L, NG, EPG, D, F), D ** -0.5),
        "w_down_e": nrm((L, NG, EPG, F, D), F ** -0.5),
        "norm3": gain((L, D)),
        "w_ple_gate": nrm((L, D, D), D ** -0.5),
        "w_ple_proj": nrm((L, PLE_DIM, D), PLE_DIM ** -0.5),
        "final_norm": gain((D,)),
    }


def reference(x, p, norm1, w_in, sgu_norm, w_spatial, b_spatial, a_re, a_im, log_dt,
              b_re, b_im, c_re, c_im, d_skip, w_glu, b_glu, out_norm_a, out_norm_b,
              w_out, norm2, w_coarse, b_coarse, w_fine, b_fine, w_gate_e, w_up_e,
              w_down_e, norm3, w_ple_gate, w_ple_proj, final_norm):
    h = x
    for i in range(DEPTH):
        hn = rmsnorm(h, norm1[i])
        proj = hn @ w_in[i]
        u = proj[..., :W_A]
        v = proj[..., W_A:2 * W_A]
        s = proj[..., 2 * W_A:]
        o_a = gmlp_mixer(u, v, sgu_norm[i], w_spatial[i], b_spatial[i])
        o_b = s5_mixer(s, a_re[i], a_im[i], log_dt[i], b_re[i], b_im[i], c_re[i], c_im[i],
                       d_skip[i], w_glu[i], b_glu[i])
        mix = jnp.concatenate([rmsnorm(o_a, out_norm_a[i]), rmsnorm(o_b, out_norm_b[i])], axis=-1)
        h = h + mix @ w_out[i]
        h = h + hier_moe(rmsnorm(h, norm2[i]), w_coarse[i], b_coarse[i], w_fine[i], b_fine[i],
                         w_gate_e[i], w_up_e[i], w_down_e[i])
        gate = jax.nn.sigmoid(rmsnorm(h, norm3[i]) @ w_ple_gate[i])
        h = h + gate * (p[i] @ w_ple_proj[i])
    return rmsnorm(h, final_norm)
```

```python
import functools

import jax
import jax.numpy as jnp
from jax import lax
from jax.experimental import pallas as pl
from jax.experimental.pallas import tpu as pltpu

F32 = jnp.float32
BF16 = jnp.bfloat16

D_MODEL = 1024
W_A = 512
W_B = 512
CHUNK = 128
HEADS_A = 8
DH_A = 64
S5_GROUPS = 32
S5_GROUP = 16
S5_STATE = 64
S5_T = 16
N_GROUPS = 4
EPG = 8
N_EXPERTS = N_GROUPS * EPG
D_EXPERT = 256
PLE_DIM = 256
EPS = 1e-6
LANES = 128
SUBLANES = 8
VMEM_LIMIT = 52 * 1024 * 1024


def _rms(x, g):
    return x * lax.rsqrt(jnp.mean(x * x, axis=-1, keepdims=True) + EPS) * g


def _dot(a, b):
    return jnp.dot(a, b, preferred_element_type=F32)


def _proj_kernel(x_ref, n1_ref, win_ref, sgu_ref, wm_ref, bias_ref, ona_ref,
                 na_ref, s_ref, sb_ref):
    tm = x_ref.shape[0]
    hn = _rms(x_ref[...], n1_ref[...]).astype(BF16)
    proj = _dot(hn, win_ref[...])
    u = jax.nn.gelu(proj[:, :W_A])
    v = jax.nn.gelu(proj[:, W_A:2 * W_A])
    s = proj[:, 2 * W_A:]
    s_ref[...] = s
    sb_ref[...] = s.astype(BF16)
    vn = _rms(v, sgu_ref[...]).astype(BF16)
    lane = lax.broadcasted_iota(jnp.int32, (CHUNK, LANES), 1)
    first_head = lane < DH_A
    zero = jnp.zeros((CHUNK, LANES), BF16)
    for c in range(tm // CHUNK):
        rows = slice(c * CHUNK, (c + 1) * CHUNK)
        zs = []
        for k in range(HEADS_A // 2):
            vp = vn[rows, k * LANES:(k + 1) * LANES]
            rhs = jnp.concatenate([jnp.where(first_head, vp, zero),
                                   jnp.where(first_head, zero, vp)], axis=0)
            zs.append(_dot(wm_ref[k], rhs))
        z = jnp.concatenate(zs, axis=1) + bias_ref[...]
        na_ref[rows, :] = _rms(u[rows, :] * z, ona_ref[...]).astype(BF16)


def _proj_call(x2, n1, win, sgu, wm, bias, ona, tm):
    n = x2.shape[0]
    full = lambda shape: pl.BlockSpec(shape, lambda i: (0,) * len(shape))
    return pl.pallas_call(
        _proj_kernel,
        grid=(n // tm,),
        in_specs=[
            pl.BlockSpec((tm, D_MODEL), lambda i: (i, 0)),
            full((1, D_MODEL)),
            full((D_MODEL, 2 * W_A + W_B)),
            full((1, W_A)),
            full((HEADS_A // 2, CHUNK, 2 * CHUNK)),
            full((CHUNK, W_A)),
            full((1, W_A)),
        ],
        out_specs=[
            pl.BlockSpec((tm, W_A), lambda i: (i, 0)),
            pl.BlockSpec((tm, W_B), lambda i: (i, 0)),
            pl.BlockSpec((tm, W_B), lambda i: (i, 0)),
        ],
        out_shape=[
            jax.ShapeDtypeStruct((n, W_A), BF16),
            jax.ShapeDtypeStruct((n, W_B), F32),
            jax.ShapeDtypeStruct((n, W_B), BF16),
        ],
        compiler_params=pltpu.CompilerParams(
            dimension_semantics=("arbitrary",), vmem_limit_bytes=VMEM_LIMIT),
        name="proj_sgu",
    )(x2, n1, win, sgu, wm, bias, ona)


def _s5_tables(a_re, a_im, log_dt, b_re, b_im, c_re, c_im):
    g, p, c, t = S5_GROUPS, S5_STATE, S5_GROUP, S5_T
    lam = lax.complex(a_re.astype(F32), a_im.astype(F32))
    dt = jnp.exp(log_dt.astype(F32))[:, None]
    z = lam * dt
    k = jnp.arange(t + 1, dtype=F32)
    apow = jnp.exp(k[:, None, None] * z[None])
    b_mat = lax.complex(b_re.astype(F32), b_im.astype(F32))
    b_bar = ((apow[1] - 1.0) / lam)[..., None] * b_mat
    c_mat = lax.complex(c_re.astype(F32), c_im.astype(F32))
    kern = jnp.einsum('gcp,kgp,gpd->kgcd', c_mat, apow[:t], b_bar).real
    tt = jnp.arange(t)
    lag = tt[None, :] - tt[:, None]
    m = kern[jnp.clip(lag, 0, t - 1)]
    m = jnp.where((lag >= 0)[:, :, None, None, None], m, 0.0)
    m = m.transpose(2, 0, 4, 1, 3).reshape(g, t * c, t * c)
    ws = apow[t - 1 - tt][:, :, :, None] * b_bar[None]
    ws = ws.transpose(1, 0, 3, 2).reshape(g, t * c, p)
    zeros = jnp.zeros((g // 2, t * c, p), F32)
    wse, wso = ws[0::2], ws[1::2]
    ws_pair = jnp.concatenate([
        jnp.concatenate([wse.real, zeros, wse.imag, zeros], axis=2),
        jnp.concatenate([zeros, wso.real, zeros, wso.imag], axis=2)], axis=1)
    ca = c_mat[None] * apow[1:t + 1][:, :, None, :]
    ca = ca.transpose(1, 3, 0, 2).reshape(g, p, t * c)
    zc = jnp.zeros((g // 2, p, t * c), F32)
    cae, cao = ca[0::2], ca[1::2]
    wcr = jnp.concatenate([jnp.concatenate([cae.real, zc], axis=2),
                           jnp.concatenate([zc, cao.real], axis=2)], axis=1)
    wci = jnp.concatenate([jnp.concatenate([-cae.imag, zc], axis=2),
                           jnp.concatenate([zc, -cao.imag], axis=2)], axis=1)
    a_t = apow[t].reshape(g // 2, 1, 2 * p)
    return (ws_pair.astype(BF16), m.astype(BF16), wcr.astype(BF16), wci.astype(BF16),
            a_t.real, a_t.imag)


def _s5_kernel(u_ref, ws_ref, m_ref, wcr_ref, wci_ref, ar_ref, ai_ref, y_ref, sre, sim):
    step = pl.program_id(0)
    npair = S5_GROUPS // 2
    nrow = sre.shape[1]

    @pl.when(step < npair)
    def _():
        uu = jnp.concatenate([u_ref[0], u_ref[1]], axis=1)
        st = _dot(uu, ws_ref[0])
        sre[step] = st[:, :LANES]
        sim[step] = st[:, LANES:]

    @pl.when(step == npair - 1)
    def _():
        low = lax.broadcasted_iota(jnp.int32, (SUBLANES, LANES), 0) < SUBLANES // 2
        half = 8
        for base in range(0, npair, half):
            ars = [jnp.broadcast_to(ar_ref[base + q], (SUBLANES, LANES)) for q in range(half)]
            ais = [jnp.broadcast_to(ai_ref[base + q], (SUBLANES, LANES)) for q in range(half)]

            def body(i, carry):
                r0 = pl.multiple_of(i * SUBLANES, SUBLANES)
                out = []
                for q in range(half):
                    xr, xi = carry[2 * q], carry[2 * q + 1]
                    tr = sre[base + q, pl.ds(r0, SUBLANES), :]
                    ti = sim[base + q, pl.ds(r0, SUBLANES), :]
                    ar, ai = ars[q], ais[q]
                    u1r = ar * xr - ai * xi + tr
                    u1i = ar * xi + ai * xr + ti
                    mr = jnp.where(low, u1r, pltpu.roll(u1r, SUBLANES // 2, 0))
                    mi = jnp.where(low, u1i, pltpu.roll(u1i, SUBLANES // 2, 0))
                    u2r = ar * mr - ai * mi + tr
                    u2i = ar * mi + ai * mr + ti
                    nr = jnp.where(low, pltpu.roll(u2r, SUBLANES // 2, 0), u2r)
                    ni = jnp.where(low, pltpu.roll(u2i, SUBLANES // 2, 0), u2i)
                    sre[base + q, pl.ds(r0, SUBLANES), :] = jnp.where(low, xr, mr)
                    sim[base + q, pl.ds(r0, SUBLANES), :] = jnp.where(low, xi, mi)
                    out += [nr, ni]
                return tuple(out)

            init = tuple(jnp.zeros((SUBLANES, LANES), F32) for _ in range(2 * half))
            lax.fori_loop(0, nrow // SUBLANES, body, init)

    @pl.when(step >= npair)
    def _():
        k = step - npair
        xr = sre[k].astype(BF16)
        xi = sim[k].astype(BF16)
        yc = _dot(xr, wcr_ref[0]) + _dot(xi, wci_ref[0])
        half = S5_T * S5_GROUP
        y_ref[0] = _dot(u_ref[0], m_ref[0]) + yc[:, :half]
        y_ref[1] = _dot(u_ref[1], m_ref[1]) + yc[:, half:]


def _s5_call(u_t, ws, m, wcr, wci, a_r, a_i):
    g, rows, width = u_t.shape
    npair = g // 2
    pair = lambda i: (i % npair, 0, 0)
    late = lambda i: (jnp.maximum(i - npair, 0), 0, 0)
    early = lambda i: (jnp.minimum(i, npair - 1), 0, 0)
    return pl.pallas_call(
        _s5_kernel,
        grid=(2 * npair,),
        in_specs=[
            pl.BlockSpec((2, rows, width), pair),
            pl.BlockSpec((1, 2 * width, 2 * LANES), early),
            pl.BlockSpec((2, width, width), late),
            pl.BlockSpec((1, LANES, 2 * width), late),
            pl.BlockSpec((1, LANES, 2 * width), late),
            pl.BlockSpec((npair, 1, LANES), lambda i: (0, 0, 0)),
            pl.BlockSpec((npair, 1, LANES), lambda i: (0, 0, 0)),
        ],
        out_specs=pl.BlockSpec((2, rows, width), late),
        out_shape=jax.ShapeDtypeStruct((g, rows, width), F32),
        scratch_shapes=[pltpu.VMEM((npair, rows, LANES), F32),
                        pltpu.VMEM((npair, rows, LANES), F32)],
        compiler_params=pltpu.CompilerParams(
            dimension_semantics=("arbitrary",), vmem_limit_bytes=VMEM_LIMIT),
        name="s5_scan",
    )(u_t, ws, m, wcr, wci, a_r, a_i)


def _mix_kernel(yc_ref, s_ref, na_ref, x_ref, dsk_ref, wglu_ref, bglu_ref, onb_ref,
                wout_ref, n2_ref, wr_ref, br_ref, h1_ref, xn_ref, comb_ref):
    tm = x_ref.shape[0]
    y = jax.nn.gelu(yc_ref[...] + dsk_ref[...] * s_ref[...])
    gl = _dot(y.astype(BF16), wglu_ref[...]) + bglu_ref[...]
    ob = gl[:, :W_B] * jax.nn.sigmoid(gl[:, W_B:])
    nb = _rms(ob, onb_ref[...]).astype(BF16)
    h1 = x_ref[...] + _dot(na_ref[...], wout_ref[0:W_A, :]) + _dot(nb, wout_ref[W_A:, :])
    h1_ref[...] = h1
    xn = _rms(h1, n2_ref[...])
    xh = xn.astype(BF16)
    xn_ref[...] = xh
    xl = (xn - xh.astype(F32)).astype(BF16)
    r1 = _dot(xh, wr_ref[...])
    r2 = _dot(xl, wr_ref[...])
    logits = r1[:, :LANES] + r1[:, LANES:] + r2[:, :LANES] + br_ref[...]
    lane = lax.broadcasted_iota(jnp.int32, (tm, LANES), 1)
    lanef = lane.astype(F32)
    neg = jnp.float32(-jnp.inf)
    big = jnp.float32(1e9)
    is_c = jnp.logical_and(lane >= N_EXPERTS, lane < N_EXPERTS + N_GROUPS)
    lc = jnp.where(is_c, logits, neg)
    mc = jnp.max(lc, axis=-1, keepdims=True)
    sc = jnp.sum(jnp.where(is_c, jnp.exp(lc - mc), 0.0), axis=-1, keepdims=True)
    p_g = 1.0 / sc
    gsel = jnp.min(jnp.where(lc == mc, lanef, big), axis=-1, keepdims=True) - N_EXPERTS
    lane_grp = (lane // EPG).astype(F32)
    is_f = jnp.logical_and(lane < N_EXPERTS, lane_grp == gsel)
    lf = jnp.where(is_f, logits, neg)
    m1 = jnp.max(lf, axis=-1, keepdims=True)
    i1 = jnp.min(jnp.where(lf == m1, lanef, big), axis=-1, keepdims=True)
    lf2 = jnp.where(lanef == i1, neg, lf)
    m2 = jnp.max(lf2, axis=-1, keepdims=True)
    i2 = jnp.min(jnp.where(lf2 == m2, lanef, big), axis=-1, keepdims=True)
    e2 = jnp.exp(m2 - m1)
    den = 1.0 + e2
    w1 = p_g / den
    w2 = p_g * e2 / den
    comb_ref[...] = jnp.where(lanef == i1, w1, 0.0) + jnp.where(lanef == i2, w2, 0.0)


def _mix_call(ycore, s, na, x2, dsk, wglu, bglu, onb, wout, n2, wr, br, tm):
    n = x2.shape[0]
    full = lambda shape: pl.BlockSpec(shape, lambda i: (0,) * len(shape))
    tile = lambda w: pl.BlockSpec((tm, w), lambda i: (i, 0))
    return pl.pallas_call(
        _mix_kernel,
        grid=(n // tm,),
        in_specs=[
            tile(W_B), tile(W_B), tile(W_A), tile(D_MODEL),
            full((1, W_B)), full((W_B, 2 * W_B)), full((1, 2 * W_B)), full((1, W_B)),
            full((D_MODEL, D_MODEL)), full((1, D_MODEL)),
            full((D_MODEL, 2 * LANES)), full((1, LANES)),
        ],
        out_specs=[tile(D_MODEL), tile(D_MODEL), tile(LANES)],
        out_shape=[
            jax.ShapeDtypeStruct((n, D_MODEL), F32),
            jax.ShapeDtypeStruct((n, D_MODEL), BF16),
            jax.ShapeDtypeStruct((n, LANES), F32),
        ],
        compiler_params=pltpu.CompilerParams(
            dimension_semantics=("arbitrary",), vmem_limit_bytes=VMEM_LIMIT),
        name="mix_router",
    )(ycore, s, na, x2, dsk, wglu, bglu, onb, wout, n2, wr, br)


def _moe_kernel(xn_ref, comb_ref, wg_ref, wu_ref, wd_ref, h1_ref, p_ref, n3_ref,
                wpg_ref, wpp_ref, fn_ref, out_ref, acc_ref):
    e = pl.program_id(1)
    tm = xn_ref.shape[0]

    @pl.when(e == 0)
    def _():
        acc_ref[...] = jnp.zeros_like(acc_ref)

    xn = xn_ref[...]
    gt = _dot(xn, wg_ref[0])
    up = _dot(xn, wu_ref[0])
    lane = lax.broadcasted_iota(jnp.int32, (tm, LANES), 1)
    ce = jnp.sum(jnp.where(lane == e, comb_ref[...], 0.0), axis=-1, keepdims=True)
    hh = (gt * jax.nn.sigmoid(gt)) * up * ce
    acc_ref[...] += _dot(hh.astype(BF16), wd_ref[0])

    @pl.when(e == pl.num_programs(1) - 1)
    def _():
        h2 = h1_ref[...] + acc_ref[...]
        gate = jax.nn.sigmoid(_dot(_rms(h2, n3_ref[...]).astype(BF16), wpg_ref[...]))
        pp = _dot(p_ref[...].astype(BF16), wpp_ref[...])
        out_ref[...] = _rms(h2 + gate * pp, fn_ref[...])


def _moe_call(xn, comb, wg, wu, wd, h1, p2, n3, wpg, wpp, fn, tm):
    n = xn.shape[0]
    full = lambda shape: pl.BlockSpec(shape, lambda i, e: (0,) * len(shape))
    tile = lambda w: pl.BlockSpec((tm, w), lambda i, e: (i, 0))
    return pl.pallas_call(
        _moe_kernel,
        grid=(n // tm, N_EXPERTS),
        in_specs=[
            tile(D_MODEL), tile(LANES),
            pl.BlockSpec((1, D_MODEL, D_EXPERT), lambda i, e: (e, 0, 0)),
            pl.BlockSpec((1, D_MODEL, D_EXPERT), lambda i, e: (e, 0, 0)),
            pl.BlockSpec((1, D_EXPERT, D_MODEL), lambda i, e: (e, 0, 0)),
            tile(D_MODEL), tile(PLE_DIM), full((1, D_MODEL)),
            full((D_MODEL, D_MODEL)), full((PLE_DIM, D_MODEL)), full((1, D_MODEL)),
        ],
        out_specs=tile(D_MODEL),
        out_shape=jax.ShapeDtypeStruct((n, D_MODEL), F32),
        scratch_shapes=[pltpu.VMEM((tm, D_MODEL), F32)],
        compiler_params=pltpu.CompilerParams(
            dimension_semantics=("arbitrary", "arbitrary"), vmem_limit_bytes=VMEM_LIMIT),
        name="moe_ple",
    )(xn, comb, wg, wu, wd, h1, p2, n3, wpg, wpp, fn)


def kernel(x, p, norm1, w_in, sgu_norm, w_spatial, b_spatial, a_re, a_im, log_dt, b_re, b_im,
           c_re, c_im, d_skip, w_glu, b_glu, out_norm_a, out_norm_b, w_out, norm2, w_coarse,
           b_coarse, w_fine, b_fine, w_gate_e, w_up_e, w_down_e, norm3, w_ple_gate, w_ple_proj,
           final_norm):
    bsz, seq, d = x.shape
    n = bsz * seq
    nchunk = seq // S5_T
    h = x.reshape(n, d)
    for i in range(p.shape[0]):
        row = lambda v: v[i].reshape(1, -1).astype(F32)
        mask = jnp.tril(jnp.ones((CHUNK, CHUNK), F32))
        wm = (w_spatial[i] * mask).astype(BF16)
        wm = wm.reshape(HEADS_A // 2, 2, CHUNK, CHUNK).transpose(0, 2, 1, 3)
        wm = wm.reshape(HEADS_A // 2, CHUNK, 2 * CHUNK)
        bias = jnp.repeat(b_spatial[i].T.astype(F32), DH_A, axis=1)
        na, s, sb = _proj_call(h, row(norm1), w_in[i].astype(BF16), row(sgu_norm), wm, bias,
                               row(out_norm_a), tm=512)
        tables = _s5_tables(a_re[i], a_im[i], log_dt[i], b_re[i], b_im[i], c_re[i], c_im[i])
        u_t = sb.reshape(bsz, nchunk, S5_T, S5_GROUPS, S5_GROUP).transpose(3, 1, 0, 2, 4)
        u_t = u_t.reshape(S5_GROUPS, nchunk * bsz, S5_T * S5_GROUP)
        y_t = _s5_call(u_t, *tables)
        ycore = y_t.reshape(S5_GROUPS, nchunk, bsz, S5_T, S5_GROUP).transpose(2, 1, 3, 0, 4)
        ycore = ycore.reshape(n, W_B)
        wr = jnp.concatenate([w_fine[i].transpose(1, 0, 2).reshape(d, N_EXPERTS), w_coarse[i],
                              jnp.zeros((d, LANES - N_EXPERTS - N_GROUPS), F32)], axis=1)
        wr_hi = wr.astype(BF16)
        wr_lo = (wr - wr_hi.astype(F32)).astype(BF16)
        br = jnp.concatenate([b_fine[i].reshape(-1), b_coarse[i],
                              jnp.zeros((LANES - N_EXPERTS - N_GROUPS,), F32)]).reshape(1, LANES)
        h1, xn, comb = _mix_call(ycore, s, na, h, row(d_skip), w_glu[i].astype(BF16), row(b_glu),
                                 row(out_norm_b), w_out[i].astype(BF16), row(norm2),
                                 jnp.concatenate([wr_hi, wr_lo], axis=1), br, tm=512)
        wg = w_gate_e[i].reshape(N_EXPERTS, d, D_EXPERT).astype(BF16)
        wu = w_up_e[i].reshape(N_EXPERTS, d, D_EXPERT).astype(BF16)
        wd = w_down_e[i].reshape(N_EXPERTS, D_EXPERT, d).astype(BF16)
        h = _moe_call(xn, comb, wg, wu, wd, h1, p[i].reshape(n, PLE_DIM), row(norm3),
                      w_ple_gate[i].astype(BF16), w_ple_proj[i].astype(BF16),
                      final_norm.reshape(1, d).astype(F32), tm=1024)
    return h.reshape(bsz, seq, d)
```
